```python
import jax, jax.numpy as jnp
from jax import lax
import numpy as np

D_MODEL = 4096
BATCH = 16
SEQ = 256
DEPTH = 4
DEC_BATCH = 8
DEC_SEQ = 1024
PAST_LEN = 512

GRID_W = 64
N_EVEN = (DEPTH + 1) // 2
N_ODD = DEPTH // 2
EPS = 1e-6
F32 = jnp.float32
ML_HEADS = 4
ML_DK = 256
ML_DV = 512
ML_CHUNK = 64
ATT_HEADS = 16
ATT_KV_HEADS = 4
ATT_GROUP = ATT_HEADS // ATT_KV_HEADS
ATT_HD = 128
WINDOW = 128
BLK = 128
ROPE_BASE = 10000.0
DN_QK_HEADS = 16
DN_V_HEADS = 32
DN_DK = 128
DN_DV = 128
DN_CHUNK = 64
CONV_W = 5
D_FF = 11008
N_EXPERTS = 8
TOP_K = 2
MOE_D_FF = 4096

EVEN_SPLITS = (ML_HEADS * ML_DK, ML_HEADS * ML_DK, ML_HEADS * ML_DV, ML_HEADS * ML_DV, 4 * ML_HEADS,
               ATT_HEADS * ATT_HD, ATT_KV_HEADS * ATT_HD, ATT_KV_HEADS * ATT_HD)
EVEN_IN = sum(EVEN_SPLITS)
EVEN_MIX = ML_HEADS * ML_DV + ATT_HEADS * ATT_HD
DN_CONV_CH = 2 * DN_QK_HEADS * DN_DK + DN_V_HEADS * DN_DV
ODD_SPLITS = (DN_CONV_CH, DN_V_HEADS * DN_DV, 2 * DN_V_HEADS, 2 * DN_V_HEADS)
ODD_IN = sum(ODD_SPLITS)

kernel_name = 'hybrid_mlstm_swa_gdn_diffusion_step'


def _split(a, sizes):
    return jnp.split(a, [int(s) for s in np.cumsum(sizes)[:-1]], axis=-1)


def _rms(x, w):
    xf = x.astype(F32)
    y = xf * lax.rsqrt(jnp.mean(xf * xf, axis=-1, keepdims=True) + EPS)
    return (y * w.astype(F32)).astype(x.dtype)


def _modulate(x, w, shift, scale):
    return _rms(x, w) * (1 + scale) + shift


def _gated_residual(x, y, w, gate):
    return x + gate * _rms(y, w)


def _heads(a, n, d):
    B, T = a.shape[:2]
    return a.astype(F32).reshape(B, T, n, d).transpose(0, 2, 1, 3)


def _l2n(a):
    return a * lax.rsqrt(jnp.sum(a * a, axis=-1, keepdims=True) + EPS)


def _flip(a):
    return jnp.flip(a, axis=2)


def _axial_rope(x):
    T = x.shape[1]
    rows_n = T // GRID_W
    row = jnp.repeat(jnp.arange(rows_n), GRID_W).astype(F32)
    col = jnp.tile(jnp.arange(GRID_W), rows_n).astype(F32)
    half = ATT_HD // 2
    nf = half // 2
    inv = ROPE_BASE ** (-jnp.arange(nf, dtype=F32) / nf)

    def rot(xp, pos):
        ang = pos[:, None] * inv[None, :]
        cos, sin = jnp.cos(ang)[None, :, None, :], jnp.sin(ang)[None, :, None, :]
        x1, x2 = xp[..., :nf], xp[..., nf:]
        return jnp.concatenate([x1 * cos - x2 * sin, x2 * cos + x1 * sin], axis=-1)

    xf = x.astype(F32)
    return jnp.concatenate([rot(xf[..., :half], row), rot(xf[..., half:], col)], axis=-1).astype(x.dtype)


def _sink_softmax(s, sink):
    sk = sink[None, :, :, None, None]
    m = jnp.maximum(jnp.max(s, axis=-1, keepdims=True), sk)
    e = jnp.exp(s - m)
    return e / (jnp.sum(e, axis=-1, keepdims=True) + jnp.exp(sk - m))


def _ctx_attention(q, k, v, sink):
    B, C = q.shape[:2]
    nb = C // BLK
    qb = q.reshape(B, nb, BLK, ATT_KV_HEADS, ATT_GROUP, ATT_HD).swapaxes(0, 1)
    scale = ATT_HD ** -0.5

    def one(qblk):
        s = jnp.einsum('bqkgd,bckd->bkgqc', qblk, k, preferred_element_type=F32) * scale
        p = _sink_softmax(s, sink).astype(v.dtype)
        return jnp.einsum('bkgqc,bckd->bqkgd', p, v)

    o = lax.map(one, qb)
    return o.swapaxes(0, 1).reshape(B, C, ATT_HEADS * ATT_HD)


def _latent_attention(q, k, v, k_ctx, v_ctx, sink):
    B, T = q.shape[:2]
    nb = T // BLK
    scale = ATT_HD ** -0.5
    qb = q.reshape(B, nb, BLK, ATT_KV_HEADS, ATT_GROUP, ATT_HD).swapaxes(0, 1)

    def windows(a):
        ap = jnp.pad(a, ((0, 0), (BLK, BLK), (0, 0), (0, 0))).reshape(B, nb + 2, BLK, ATT_KV_HEADS, ATT_HD)
        w = jnp.concatenate([ap[:, :-2], ap[:, 1:-1], ap[:, 2:]], axis=2)
        return w.swapaxes(0, 1)

    kw, vw = windows(k), windows(v)
    qpos = jnp.arange(nb)[:, None] * BLK + jnp.arange(BLK)[None, :]
    kpos = (jnp.arange(nb)[:, None] - 1) * BLK + jnp.arange(3 * BLK)[None, :]
    mask = ((jnp.abs(qpos[:, :, None] - kpos[:, None, :]) <= WINDOW)
            & (kpos[:, None, :] >= 0) & (kpos[:, None, :] < T))
    C = k_ctx.shape[1]

    def one(args):
        qblk, kblk, vblk, mblk = args
        s_ctx = jnp.einsum('bqkgd,bckd->bkgqc', qblk, k_ctx, preferred_element_type=F32) * scale
        s_loc = jnp.einsum('bqkgd,bjkd->bkgqj', qblk, kblk, preferred_element_type=F32) * scale
        s_loc = jnp.where(mblk[None, None, None], s_loc, -jnp.inf)
        p = _sink_softmax(jnp.concatenate([s_ctx, s_loc], axis=-1), sink).astype(v.dtype)
        return (jnp.einsum('bkgqc,bckd->bqkgd', p[..., :C], v_ctx)
                + jnp.einsum('bkgqj,bjkd->bqkgd', p[..., C:], vblk))

    o = lax.map(one, (qb, kw, vw, mask))
    return o.swapaxes(0, 1).reshape(B, T, ATT_HEADS * ATT_HD)


def _mlstm_dir(q, k, v, ig, lf, c0, n0, m0):
    B, H, T = q.shape[:3]
    nc = T // ML_CHUNK
    tril = jnp.tril(jnp.ones((ML_CHUNK, ML_CHUNK), bool))

    def chunks(a):
        return jnp.moveaxis(a.reshape(B, H, nc, ML_CHUNK, *a.shape[3:]), 2, 0)

    def step(carry, xs):
        c, n, m = carry
        qc, kc, vc, ic, fc = xs
        b = jnp.cumsum(fc, axis=-1)
        d = jnp.where(tril, b[..., :, None] - b[..., None, :] + ic[..., None, :], -jnp.inf)
        inter = b + m[..., None]
        m_t = jnp.maximum(inter, jnp.max(d, axis=-1))
        w_intra = jnp.exp(d - m_t[..., None])
        w_inter = jnp.exp(inter - m_t)
        qk = jnp.einsum('bhld,bhsd->bhls', qc, kc) * w_intra
        num = jnp.einsum('bhls,bhse->bhle', qk, vc) + jnp.einsum('bhld,bhde->bhle', qc, c) * w_inter[..., None]
        den = jnp.sum(qk, axis=-1) + jnp.einsum('bhld,bhd->bhl', qc, n) * w_inter
        h = num / jnp.maximum(jnp.abs(den), jnp.exp(-m_t))[..., None]
        b_last = b[..., -1]
        last = b_last[..., None] - b + ic
        m_new = jnp.maximum(b_last + m, jnp.max(last, axis=-1))
        keep = jnp.exp(b_last + m - m_new)
        w_s = jnp.exp(last - m_new[..., None])
        c_new = keep[..., None, None] * c + jnp.einsum('bhsd,bhse->bhde', kc * w_s[..., None], vc)
        n_new = keep[..., None] * n + jnp.einsum('bhs,bhsd->bhd', w_s, kc)
        return (c_new, n_new, m_new), h

    (c, n, m), h = lax.scan(step, (c0, n0, m0), tuple(chunks(a) for a in (q, k, v, ig, lf)))
    return jnp.moveaxis(h, 0, 2).reshape(B, H, T, -1), c, n, m


def _mlstm_bidir(q, k, v, ig, lf, c0, n0, m0):
    h_f, c_f, n_f, m_f = _mlstm_dir(q, k, v, ig[..., 0], lf[..., 0], c0[:, 0], n0[:, 0], m0[:, 0])
    h_b, c_b, n_b, m_b = _mlstm_dir(_flip(q), _flip(k), _flip(v), _flip(ig[..., 1]), _flip(lf[..., 1]),
                                    c0[:, 1], n0[:, 1], m0[:, 1])
    return (h_f + _flip(h_b), jnp.stack([c_f, c_b], axis=1), jnp.stack([n_f, n_b], axis=1),
            jnp.stack([m_f, m_b], axis=1))


def _delta_dir(q, k, v, beta, g, s0):
    B, H, T = q.shape[:3]
    L = DN_CHUNK
    nc = T // L
    tril = jnp.tril(jnp.ones((L, L), bool))
    strict = jnp.tril(jnp.ones((L, L), bool), -1)
    eye = jnp.eye(L, dtype=F32)

    def chunks(a):
        return jnp.moveaxis(a.reshape(B, H, nc, L, *a.shape[3:]), 2, 0)

    def step(s, xs):
        qc, kc, vc, bc, gc = xs
        gcs = jnp.cumsum(gc, axis=-1)
        decay = jnp.exp(jnp.where(tril, gcs[..., :, None] - gcs[..., None, :], -jnp.inf))
        kb = kc * bc[..., None]
        a_mat = jnp.where(strict, jnp.einsum('bhld,bhsd->bhls', kb, kc) * decay, 0.0)
        rhs = jnp.concatenate([vc * bc[..., None], kb * jnp.exp(gcs)[..., None]], axis=-1)
        sol = lax.linalg.triangular_solve(a_mat + eye, rhs, left_side=True, lower=True, unit_diagonal=True)
        u, w = sol[..., :DN_DV], sol[..., DN_DV:]
        v_new = u - w @ s
        intra = jnp.einsum('bhld,bhsd->bhls', qc, kc) * decay
        o = (qc * jnp.exp(gcs)[..., None]) @ s + intra @ v_new
        g_last = gcs[..., -1]
        s_new = (s * jnp.exp(g_last)[..., None, None]
                 + jnp.einsum('bhsd,bhse->bhde', kc * jnp.exp(g_last[..., None] - gcs)[..., None], v_new))
        return s_new, o

    s, o = lax.scan(step, s0, tuple(chunks(a) for a in (q, k, v, beta, g)))
    return jnp.moveaxis(o, 0, 2).reshape(B, H, T, DN_DV), s


def _centred_conv(x, w):
    ch = x.shape[-1]
    return lax.conv_general_dilated(x, w[:, None, :].astype(x.dtype), window_strides=(1,),
                                    padding=[(CONV_W // 2, CONV_W // 2)],
                                    dimension_numbers=('NWC', 'WIO', 'NWC'), feature_group_count=ch)


def _even_mixer(h, w_in, gate_b, mnorm_w, sink, w_out, ctx):
    B, T, _ = h.shape
    qm, km, vm, om, gm, qa, ka, va = _split(h @ w_in, EVEN_SPLITS)
    qm = _heads(qm, ML_HEADS, ML_DK) * (ML_DK ** -0.5)
    km = _heads(km, ML_HEADS, ML_DK)
    vm = _heads(vm, ML_HEADS, ML_DV)
    g = gm.astype(F32).reshape(B, T, 2, 2, ML_HEADS) + gate_b.astype(F32)
    g = g.transpose(0, 4, 1, 2, 3)
    ig, lf = g[..., 0], jax.nn.log_sigmoid(g[..., 1])
    qa = qa.reshape(B, T, ATT_HEADS, ATT_HD)
    ka = ka.reshape(B, T, ATT_KV_HEADS, ATT_HD)
    va = va.reshape(B, T, ATT_KV_HEADS, ATT_HD)
    sink = sink.astype(F32).reshape(ATT_KV_HEADS, ATT_GROUP)
    if ctx is None:
        c0 = jnp.zeros((B, 2, ML_HEADS, ML_DK, ML_DV), F32)
        n0 = jnp.zeros((B, 2, ML_HEADS, ML_DK), F32)
        m0 = jnp.zeros((B, 2, ML_HEADS), F32)
        ha = _ctx_attention(qa, ka, va, sink)
    else:
        k_ctx, v_ctx, c0, n0, m0 = ctx
        c0, n0, m0 = c0.astype(F32), n0.astype(F32), m0.astype(F32)
        ha = _latent_attention(_axial_rope(qa), _axial_rope(ka), va, k_ctx, v_ctx, sink)
    hm, c_fin, n_fin, m_fin = _mlstm_bidir(qm, km, vm, ig, lf, c0, n0, m0)
    hm = hm.transpose(0, 2, 1, 3)
    hm = hm * lax.rsqrt(jnp.mean(hm * hm, axis=-1, keepdims=True) + EPS)
    hm = hm.reshape(B, T, ML_HEADS * ML_DV) * mnorm_w.astype(F32) * jax.nn.sigmoid(om.astype(F32))
    y = jnp.concatenate([hm.astype(h.dtype), ha.astype(h.dtype)], axis=-1) @ w_out
    new = (ka, va, c_fin, n_fin, m_fin) if ctx is None else None
    return y, new


def _odd_mixer(h, w_in, conv_w, a_log, dt_bias, dnorm_w, w_out, ctx):
    B, T, _ = h.shape
    qkv, z, a, b = _split(h @ w_in, ODD_SPLITS)
    qkv = jax.nn.silu(_centred_conv(qkv, conv_w))
    q, k, v = _split(qkv, (DN_QK_HEADS * DN_DK, DN_QK_HEADS * DN_DK, DN_V_HEADS * DN_DV))
    rep = DN_V_HEADS // DN_QK_HEADS
    q = jnp.repeat(_l2n(_heads(q, DN_QK_HEADS, DN_DK)), rep, axis=1) * (DN_DK ** -0.5)
    k = jnp.repeat(_l2n(_heads(k, DN_QK_HEADS, DN_DK)), rep, axis=1)
    v = _heads(v, DN_V_HEADS, DN_DV)
    a = a.astype(F32).reshape(B, T, 2, DN_V_HEADS)
    b = b.astype(F32).reshape(B, T, 2, DN_V_HEADS)
    g = -jnp.exp(a_log.astype(F32)) * jax.nn.softplus(a + dt_bias.astype(F32))
    beta = jax.nn.sigmoid(b)
    g, beta = g.transpose(0, 3, 1, 2), beta.transpose(0, 3, 1, 2)
    if ctx is None:
        s0 = jnp.zeros((B, 2, DN_V_HEADS, DN_DK, DN_DV), F32)
    else:
        s0 = ctx.astype(F32)
    o_f, s_f = _delta_dir(q, k, v, beta[..., 0], g[..., 0], s0[:, 0])
    o_b, s_b = _delta_dir(_flip(q), _flip(k), _flip(v), _flip(beta[..., 1]), _flip(g[..., 1]), s0[:, 1])
    o = (o_f + _flip(o_b)).transpose(0, 2, 1, 3)
    o = o * lax.rsqrt(jnp.mean(o * o, axis=-1, keepdims=True) + EPS) * dnorm_w.astype(F32)
    o = o * jax.nn.silu(z.astype(F32).reshape(B, T, DN_V_HEADS, DN_DV))
    y = o.reshape(B, T, DN_V_HEADS * DN_DV).astype(h.dtype) @ w_out
    new = jnp.stack([s_f, s_b], axis=1) if ctx is None else None
    return y, new


def _swiglu(h, w_gate, w_up, w_down):
    return (jax.nn.silu(h @ w_gate) * (h @ w_up)) @ w_down


def _moe(h, router, w_gate, w_up, w_down):
    logits = jnp.einsum('btd,de->bte', h, router, preferred_element_type=F32)
    probs = jax.nn.softmax(logits, axis=-1)
    top_p, top_i = lax.top_k(probs, TOP_K)
    top_p = top_p / jnp.sum(top_p, axis=-1, keepdims=True)
    gates = jnp.sum(jax.nn.one_hot(top_i, N_EXPERTS, dtype=F32) * top_p[..., None], axis=-2).astype(h.dtype)
    out = jnp.zeros_like(h)
    for e in range(N_EXPERTS):
        out = out + gates[..., e:e + 1] * _swiglu(h, w_gate[e], w_up[e], w_down[e])
    return out


def setup_inputs(seed: int = 0) -> dict:
    key = jax.random.key(seed)
    it = iter(jax.random.split(key, 40))
    D = D_MODEL
    sd = D ** -0.5

    def nrm(shape, std):
        return jax.random.normal(next(it), shape, F32) * std

    x_prompt = nrm((BATCH, SEQ, D), 1.0)
    x_sample = nrm((DEC_BATCH, DEC_SEQ, D), 1.0)
    cache_attn_k = nrm((DEC_BATCH, N_EVEN, PAST_LEN, ATT_KV_HEADS, ATT_HD), 1.0)
    cache_attn_v = nrm((DEC_BATCH, N_EVEN, PAST_LEN, ATT_KV_HEADS, ATT_HD), 1.0)
    state_mlstm_c = nrm((DEC_BATCH, N_EVEN, 2, ML_HEADS, ML_DK, ML_DV), 0.5)
    state_mlstm_n = nrm((DEC_BATCH, N_EVEN, 2, ML_HEADS, ML_DK), 0.5)
    state_mlstm_m = nrm((DEC_BATCH, N_EVEN, 2, ML_HEADS), 1.0)
    state_delta = nrm((DEC_BATCH, N_ODD, 2, DN_V_HEADS, DN_DK, DN_DV), 0.1)
    c = nrm((DEC_BATCH, D), 1.0)
    c_ctx = nrm((D,), 1.0)
    w_mod = nrm((DEPTH, D, 6 * D), 0.5 * sd)
    b_mod = nrm((DEPTH, 6 * D), 0.02)
    norm_w = 1.0 + nrm((DEPTH, 4, D), 0.05)
    even_w_in = nrm((N_EVEN, D, EVEN_IN), sd)
    even_gate_b = nrm((N_EVEN, 2, 2, ML_HEADS), 0.5) + jnp.array([0.0, 3.0], F32)[None, None, :, None]
    even_mnorm_w = 1.0 + nrm((N_EVEN, ML_HEADS * ML_DV), 0.05)
    even_sink = nrm((N_EVEN, ATT_HEADS), 1.0)
    even_w_out = nrm((N_EVEN, EVEN_MIX, D), EVEN_MIX ** -0.5)
    ffn_w_gate = nrm((N_EVEN, D, D_FF), sd)
    ffn_w_up = nrm((N_EVEN, D, D_FF), sd)
    ffn_w_down = nrm((N_EVEN, D_FF, D), D_FF ** -0.5)
    odd_w_in = nrm((N_ODD, D, ODD_IN), sd)
    odd_conv_w = nrm((N_ODD, CONV_W, DN_CONV_CH), CONV_W ** -0.5)
    odd_a_log = jnp.log(jax.random.uniform(next(it), (N_ODD, 2, DN_V_HEADS), F32, 1.0, 16.0))
    dt = jax.random.uniform(next(it), (N_ODD, 2, DN_V_HEADS), F32, 0.001, 0.1)
    odd_dt_bias = jnp.log(jnp.expm1(dt))
    odd_dnorm_w = 1.0 + nrm((N_ODD, DN_DV), 0.05)
    odd_w_out = nrm((N_ODD, DN_V_HEADS * DN_DV, D), (DN_V_HEADS * DN_DV) ** -0.5)
    moe_router = nrm((N_ODD, D, N_EXPERTS), sd)
    moe_w_gate = nrm((N_ODD, N_EXPERTS, D, MOE_D_FF), sd)
    moe_w_up = nrm((N_ODD, N_EXPERTS, D, MOE_D_FF), sd)
    moe_w_down = nrm((N_ODD, N_EXPERTS, MOE_D_FF, D), MOE_D_FF ** -0.5)
    return {'x_prompt': x_prompt, 'x_sample': x_sample,
            'cache_attn_k': cache_attn_k, 'cache_attn_v': cache_attn_v,
            'state_mlstm_c': state_mlstm_c, 'state_mlstm_n': state_mlstm_n, 'state_mlstm_m': state_mlstm_m,
            'state_delta': state_delta, 'c': c, 'c_ctx': c_ctx, 'w_mod': w_mod, 'b_mod': b_mod, 'norm_w': norm_w,
            'even_w_in': even_w_in, 'even_gate_b': even_gate_b, 'even_mnorm_w': even_mnorm_w,
            'even_sink': even_sink, 'even_w_out': even_w_out,
            'ffn_w_gate': ffn_w_gate, 'ffn_w_up': ffn_w_up, 'ffn_w_down': ffn_w_down,
            'odd_w_in': odd_w_in, 'odd_conv_w': odd_conv_w, 'odd_a_log': odd_a_log, 'odd_dt_bias': odd_dt_bias,
            'odd_dnorm_w': odd_dnorm_w, 'odd_w_out': odd_w_out,
            'moe_router': moe_router, 'moe_w_gate': moe_w_gate, 'moe_w_up': moe_w_up, 'moe_w_down': moe_w_down}


def reference(x_prompt, x_sample, cache_attn_k, cache_attn_v, state_mlstm_c, state_mlstm_n, state_mlstm_m,
              state_delta, c, c_ctx, w_mod, b_mod, norm_w, even_w_in, even_gate_b, even_mnorm_w, even_sink,
              even_w_out, ffn_w_gate, ffn_w_up, ffn_w_down, odd_w_in, odd_conv_w, odd_a_log, odd_dt_bias,
              odd_dnorm_w, odd_w_out, moe_router, moe_w_gate, moe_w_up, moe_w_down):
    xp, xs = x_prompt, x_sample
    new_k, new_v, new_c, new_n, new_m, new_s = [], [], [], [], [], []
    for li in range(DEPTH):
        mod_p = (jax.nn.silu(c_ctx) @ w_mod[li] + b_mod[li])[None, None, :]
        mod_s = (jax.nn.silu(c) @ w_mod[li] + b_mod[li])[:, None, :]
        sh1p, sc1p, gt1p, sh2p, sc2p, gt2p = jnp.split(mod_p, 6, axis=-1)
        sh1s, sc1s, gt1s, sh2s, sc2s, gt2s = jnp.split(mod_s, 6, axis=-1)
        hp = _modulate(xp, norm_w[li, 0], sh1p, sc1p)
        hs = _modulate(xs, norm_w[li, 0], sh1s, sc1s)
        if li % 2 == 0:
            e = li // 2
            ew = (even_w_in[e], even_gate_b[e], even_mnorm_w[e], even_sink[e], even_w_out[e])
            yp, (k_e, v_e, c_e, n_e, m_e) = _even_mixer(hp, *ew, None)
            ys, _ = _even_mixer(hs, *ew, (cache_attn_k[:, e], cache_attn_v[:, e], state_mlstm_c[:, e],
                                           state_mlstm_n[:, e], state_mlstm_m[:, e]))
            new_k.append(k_e)
            new_v.append(v_e)
            new_c.append(c_e)
            new_n.append(n_e)
            new_m.append(m_e)
        else:
            o = li // 2
            ow = (odd_w_in[o], odd_conv_w[o], odd_a_log[o], odd_dt_bias[o], odd_dnorm_w[o], odd_w_out[o])
            yp, s_o = _odd_mixer(hp, *ow, None)
            ys, _ = _odd_mixer(hs, *ow, state_delta[:, o])
            new_s.append(s_o)
        xp = _gated_residual(xp, yp, norm_w[li, 1], gt1p)
        xs = _gated_residual(xs, ys, norm_w[li, 1], gt1s)
        hp = _modulate(xp, norm_w[li, 2], sh2p, sc2p)
        hs = _modulate(xs, norm_w[li, 2], sh2s, sc2s)
        if li % 2 == 0:
            fw = (ffn_w_gate[li // 2], ffn_w_up[li // 2], ffn_w_down[li // 2])
            yp, ys = _swiglu(hp, *fw), _swiglu(hs, *fw)
        else:
            mw = (moe_router[li // 2], moe_w_gate[li // 2], moe_w_up[li // 2], moe_w_down[li // 2])
            yp, ys = _moe(hp, *mw), _moe(hs, *mw)
        xp = _gated_residual(xp, yp, norm_w[li, 3], gt2p)
        xs = _gated_residual(xs, ys, norm_w[li, 3], gt2s)
    new_attn_k = jnp.stack(new_k, axis=1)
    new_attn_v = jnp.stack(new_v, axis=1)
    new_mlstm_c = jnp.stack(new_c, axis=1)
    new_mlstm_n = jnp.stack(new_n, axis=1)
    new_mlstm_m = jnp.stack(new_m, axis=1)
    new_delta = jnp.stack(new_s, axis=1)
    return (xp, xs, new_attn_k, new_attn_v, new_mlstm_c, new_mlstm_n, new_mlstm_m, new_delta)
```

```python
import functools

import numpy as np
import jax
import jax.numpy as jnp
from jax import lax
from jax.experimental import pallas as pl
from jax.experimental.pallas import tpu as pltpu

F32 = jnp.float32
BF16 = jnp.bfloat16
EPS = 1e-6

V7X_VMEM_BYTES = 64 * 1024 * 1024
VMEM_LIMIT = V7X_VMEM_BYTES - 8 * 1024 * 1024
LANES = 128

GRID_W = 64
ML_HEADS, ML_DK, ML_DV, ML_CHUNK = 4, 256, 512, 64
ATT_HEADS, ATT_KV_HEADS, ATT_HD = 16, 4, 128
ATT_GROUP = ATT_HEADS // ATT_KV_HEADS
WINDOW, BLK = 128, 128
ROPE_BASE = 10000.0
DN_QK_HEADS, DN_V_HEADS, DN_DK, DN_DV, DN_CHUNK = 16, 32, 128, 128, 64
DN_GROUP = 128
CONV_W = 5
N_EXPERTS, TOP_K = 8, 2


def _cparams(*sem):
    return pltpu.CompilerParams(dimension_semantics=sem, vmem_limit_bytes=VMEM_LIMIT)


def _dot(a, b):
    return jnp.dot(a, b, preferred_element_type=F32)


def _dot_nt(a, b):
    return lax.dot_general(a, b, (((1,), (1,)), ((), ())), preferred_element_type=F32)


def _dot_tn(a, b):
    return lax.dot_general(a, b, (((0,), (0,)), ((), ())), preferred_element_type=F32)


def _split_bf16(x):
    hi = x.astype(BF16)
    lo = (x - hi.astype(F32)).astype(BF16)
    return hi, lo


def _dot3(a, b):
    ah, al = _split_bf16(a)
    bh, bl = _split_bf16(b)
    return _dot(ah, bh) + (_dot(ah, bl) + _dot(al, bh))


def _sigmoid(x):
    return 1.0 / (1.0 + jnp.exp(-x))


def _softplus(x):
    return jnp.maximum(x, 0.0) + jnp.log1p(jnp.exp(-jnp.abs(x)))


def _rms_rows(x):
    return x * lax.rsqrt(jnp.mean(x * x, axis=-1, keepdims=True) + EPS)


def _mod_kernel(c_ref, w_ref, b_ref, o_ref):
    cs = c_ref[...]
    cs = cs * _sigmoid(cs)
    o_ref[0] = _dot(cs.astype(BF16), w_ref[0].astype(BF16)) + b_ref[0]


def _mod_table(cc, w_mod, b_mod, tn=512):
    depth, d, n = w_mod.shape
    r = cc.shape[0]
    return pl.pallas_call(
        _mod_kernel,
        grid=(depth, n // tn),
        in_specs=[pl.BlockSpec((r, d), lambda l, j: (0, 0)),
                  pl.BlockSpec((1, d, tn), lambda l, j: (l, 0, j)),
                  pl.BlockSpec((1, 1, tn), lambda l, j: (l, 0, j))],
        out_specs=pl.BlockSpec((1, r, tn), lambda l, j: (l, 0, j)),
        out_shape=jax.ShapeDtypeStruct((depth, r, n), F32),
        compiler_params=_cparams("parallel", "parallel"),
        name="mod_table",
    )(cc, w_mod, b_mod.reshape(depth, 1, n))


class _Rows:
    def __init__(self, n_prompt_rows, sample_seq, tile):
        self.tile = tile
        self.prompt_tiles = n_prompt_rows // tile
        self.tiles_per_sample = sample_seq // tile

    def mod_row(self, i):
        return jnp.where(i < self.prompt_tiles, 0, 1 + (i - self.prompt_tiles) // self.tiles_per_sample)


def _norm_mod_kernel(x_ref, w_ref, sh_ref, sc_ref, o_ref):
    y = _rms_rows(x_ref[...]) * w_ref[...]
    o_ref[...] = (y * (1.0 + sc_ref[0]) + sh_ref[0]).astype(o_ref.dtype)


def _norm_modulate(x, w, mod, rows, k_shift, k_scale):
    m, d = x.shape
    t = rows.tile
    return pl.pallas_call(
        _norm_mod_kernel,
        grid=(m // t,),
        in_specs=[pl.BlockSpec((t, d), lambda i: (i, 0)),
                  pl.BlockSpec((1, d), lambda i: (0, 0)),
                  pl.BlockSpec((1, 1, d), lambda i: (rows.mod_row(i) * 6 + k_shift, 0, 0)),
                  pl.BlockSpec((1, 1, d), lambda i: (rows.mod_row(i) * 6 + k_scale, 0, 0))],
        out_specs=pl.BlockSpec((t, d), lambda i: (i, 0)),
        out_shape=jax.ShapeDtypeStruct((m, d), BF16),
        compiler_params=_cparams("parallel"),
        name="norm_modulate",
    )(x, w.reshape(1, d), mod, mod)


def _gated_res_kernel(x_ref, y_ref, w_ref, g_ref, o_ref):
    o_ref[...] = x_ref[...] + g_ref[0] * (_rms_rows(y_ref[...]) * w_ref[...])


def _gated_residual(x, y, w, mod, rows, k_gate):
    m, d = x.shape
    t = rows.tile
    return pl.pallas_call(
        _gated_res_kernel,
        grid=(m // t,),
        in_specs=[pl.BlockSpec((t, d), lambda i: (i, 0)),
                  pl.BlockSpec((t, d), lambda i: (i, 0)),
                  pl.BlockSpec((1, d), lambda i: (0, 0)),
                  pl.BlockSpec((1, 1, d), lambda i: (rows.mod_row(i) * 6 + k_gate, 0, 0))],
        out_specs=pl.BlockSpec((t, d), lambda i: (i, 0)),
        out_shape=jax.ShapeDtypeStruct((m, d), F32),
        compiler_params=_cparams("parallel"),
        name="gated_residual",
    )(x, y, w.reshape(1, d), mod)


def _mm_kernel(a_ref, b_ref, o_ref, *scratch, nk):
    part = _dot(a_ref[...], b_ref[...])
    if nk == 1:
        o_ref[...] = part.astype(o_ref.dtype)
        return
    acc_ref, = scratch
    k = pl.program_id(2)

    @pl.when(k == 0)
    def _():
        acc_ref[...] = part

    @pl.when(k > 0)
    def _():
        acc_ref[...] += part

    @pl.when(k == nk - 1)
    def _():
        o_ref[...] = acc_ref[...].astype(o_ref.dtype)


def _matmul(a, b, out_dtype, tm, tn, tk=None):
    m, k = a.shape
    n = b.shape[1]
    tk = k if tk is None else tk
    nk = k // tk
    assert m % tm == 0 and n % tn == 0 and k % tk == 0
    return pl.pallas_call(
        functools.partial(_mm_kernel, nk=nk),
        grid=(m // tm, n // tn, nk),
        in_specs=[pl.BlockSpec((tm, tk), lambda i, j, kk: (i, kk)),
                  pl.BlockSpec((tk, tn), lambda i, j, kk: (kk, j))],
        out_specs=pl.BlockSpec((tm, tn), lambda i, j, kk: (i, j)),
        out_shape=jax.ShapeDtypeStruct((m, n), out_dtype),
        scratch_shapes=[] if nk == 1 else [pltpu.VMEM((tm, tn), F32)],
        compiler_params=_cparams("parallel", "parallel", "arbitrary"),
        name="matmul",
    )(a, b)


def _swiglu_up_kernel(h_ref, wg_ref, wu_ref, o_ref):
    h = h_ref[...]
    g = _dot(h, wg_ref[0])
    u = _dot(h, wu_ref[0])
    o_ref[...] = (g * _sigmoid(g) * u).astype(o_ref.dtype)


def _swiglu_up(h, wg, wu, tm, tf):
    m, k = h.shape
    e, _, f = wg.shape
    nf = f // tf
    assert m % tm == 0 and f % tf == 0
    return pl.pallas_call(
        _swiglu_up_kernel,
        grid=(m // tm, e, nf),
        in_specs=[pl.BlockSpec((tm, k), lambda i, ee, j: (i, 0)),
                  pl.BlockSpec((1, k, tf), lambda i, ee, j: (ee, 0, j)),
                  pl.BlockSpec((1, k, tf), lambda i, ee, j: (ee, 0, j))],
        out_specs=pl.BlockSpec((tm, tf), lambda i, ee, j: (i, ee * nf + j)),
        out_shape=jax.ShapeDtypeStruct((m, e * f), BF16),
        compiler_params=_cparams("parallel", "parallel", "parallel"),
        name="swiglu_up",
    )(h, wg, wu)


def _router_kernel(h_ref, rh_ref, rl_ref, o_ref, *, n_experts):
    h = h_ref[...]
    logits = _dot(h, rh_ref[...]) + _dot(h, rl_ref[...])
    lane = lax.broadcasted_iota(jnp.int32, logits.shape, 1).astype(F32)
    valid = lane < n_experts
    lg = jnp.where(valid, logits, -jnp.inf)
    ex = jnp.exp(lg - jnp.max(lg, axis=1, keepdims=True))
    probs = ex / jnp.sum(ex, axis=1, keepdims=True)
    p1 = jnp.max(probs, axis=1, keepdims=True)
    i1 = jnp.min(jnp.where((probs == p1) & valid, lane, float(LANES)), axis=1, keepdims=True)
    rest = jnp.where((lane == i1) | (lane >= n_experts), -1.0, probs)
    p2 = jnp.max(rest, axis=1, keepdims=True)
    i2 = jnp.min(jnp.where((rest == p2) & valid, lane, float(LANES)), axis=1, keepdims=True)
    den = p1 + p2
    o_ref[...] = jnp.where(lane == i1, p1 / den, 0.0) + jnp.where(lane == i2, p2 / den, 0.0)


def _moe_gates(h, router, tm=512):
    m, d = h.shape
    e = router.shape[1]
    rp = jnp.zeros((d, LANES), F32).at[:, :e].set(router)
    rh = rp.astype(BF16)
    rl = (rp - rh.astype(F32)).astype(BF16)
    return pl.pallas_call(
        functools.partial(_router_kernel, n_experts=e),
        grid=(m // tm,),
        in_specs=[pl.BlockSpec((tm, d), lambda i: (i, 0)),
                  pl.BlockSpec((d, LANES), lambda i: (0, 0)),
                  pl.BlockSpec((d, LANES), lambda i: (0, 0))],
        out_specs=pl.BlockSpec((tm, LANES), lambda i: (i, 0)),
        out_shape=jax.ShapeDtypeStruct((m, LANES), F32),
        compiler_params=_cparams("parallel"),
        name="moe_router",
    )(h, rh, rl)


def _moe_down_kernel(a_ref, w_ref, g_ref, o_ref, acc_ref, *, n_experts):
    e = pl.program_id(2)
    gates = g_ref[...]
    lane = lax.broadcasted_iota(jnp.int32, gates.shape, 1)
    g = jnp.sum(jnp.where(lane == e, gates, 0.0), axis=1, keepdims=True)
    part = g * _dot(a_ref[...], w_ref[0])

    @pl.when(e == 0)
    def _():
        acc_ref[...] = part

    @pl.when(e > 0)
    def _():
        acc_ref[...] += part

    @pl.when(e == n_experts - 1)
    def _():
        o_ref[...] = acc_ref[...]


def _moe_down(act, wd, gates, tm, tn):
    m = act.shape[0]
    e, f, d = wd.shape
    return pl.pallas_call(
        functools.partial(_moe_down_kernel, n_experts=e),
        grid=(m // tm, d // tn, e),
        in_specs=[pl.BlockSpec((tm, f), lambda i, j, ee: (i, ee)),
                  pl.BlockSpec((1, f, tn), lambda i, j, ee: (ee, 0, j)),
                  pl.BlockSpec((tm, LANES), lambda i, j, ee: (i, 0))],
        out_specs=pl.BlockSpec((tm, tn), lambda i, j, ee: (i, j)),
        out_shape=jax.ShapeDtypeStruct((m, d), F32),
        scratch_shapes=[pltpu.VMEM((tm, tn), F32)],
        compiler_params=_cparams("parallel", "parallel", "arbitrary"),
        name="moe_down",
    )(act, wd, gates)


def _mlstm_kernel(gb_ref, q_ref, k_ref, v_ref, om_ref, g_ref, mw_ref, *rest, seq, has_init):
    if has_init:
        c0_ref, n0_ref, m0_ref, h_ref, c_ref, n_ref, m_ref, hf_ref, hb_ref = rest
    else:
        h_ref, c_ref, n_ref, m_ref, hf_ref, hb_ref = rest
    L = ML_CHUNK
    nc = seq // L
    head = pl.program_id(1)
    r = lax.broadcasted_iota(jnp.int32, (L, L), 0)
    c = lax.broadcasted_iota(jnp.int32, (L, L), 1)
    eye = r == c
    causal = (c <= r, c >= r)
    qscale = ML_DK ** -0.5

    def to_col(row):
        return jnp.sum(jnp.where(eye, jnp.broadcast_to(row, (L, L)), 0.0), axis=1, keepdims=True)

    def to_row(col):
        return jnp.sum(jnp.where(eye, jnp.broadcast_to(col, (L, L)), 0.0), axis=0, keepdims=True)

    if has_init:
        c_ref[...] = c0_ref[...]
        n_ref[...] = n0_ref[...]
        m_ref[...] = m0_ref[...]
    else:
        c_ref[...] = jnp.zeros_like(c_ref)
        n_ref[...] = jnp.zeros_like(n_ref)
        m_ref[...] = jnp.zeros_like(m_ref)

    def chunk(d, j, out_ref):
        sl = pl.ds(pl.multiple_of(j * L, L), L)
        q = q_ref[sl, :] * qscale
        k = k_ref[sl, :]
        v = v_ref[sl, :]
        gi = d * 2 * ML_HEADS + head
        gf = gi + ML_HEADS
        i_row = g_ref[gi, pl.ds(j, 1), :] + gb_ref[gi]
        f_raw = g_ref[gf, pl.ds(j, 1), :] + gb_ref[gf]
        f_row = -_softplus(-f_raw)
        cst = c_ref[d]
        nst = n_ref[d]
        mst = m_ref[d]
        mask = causal[d]
        b_col = jnp.sum(jnp.where(mask, jnp.broadcast_to(f_row, (L, L)), 0.0), axis=1, keepdims=True)
        b_row = to_row(b_col)
        dmat = jnp.where(mask, b_col - b_row + i_row, -jnp.inf)
        inter = b_col + mst
        m_t = jnp.maximum(inter, jnp.max(dmat, axis=1, keepdims=True))
        w_intra = jnp.exp(dmat - m_t)
        w_inter = jnp.exp(inter - m_t)
        qb = q.astype(BF16)
        kb = k.astype(BF16)
        vb = v.astype(BF16)
        qk = _dot_nt(qb, kb) * w_intra
        num = _dot(qk.astype(BF16), vb) + _dot(qb, cst.astype(BF16)) * w_inter
        den = jnp.sum(qk, axis=1, keepdims=True) + jnp.sum(q * nst, axis=1, keepdims=True) * w_inter
        out_ref[sl, :] = num / jnp.maximum(jnp.abs(den), jnp.exp(-m_t))
        b_last = jnp.sum(f_row, axis=1, keepdims=True)
        last_row = b_last - b_row + i_row
        m_new = jnp.maximum(b_last + mst, jnp.max(last_row, axis=1, keepdims=True))
        keep = jnp.exp(b_last + mst - m_new)
        kw = k * to_col(jnp.exp(last_row - m_new))
        c_ref[d] = keep * cst + _dot_tn(kw.astype(BF16), vb)
        n_ref[d] = keep * nst + jnp.sum(kw, axis=0, keepdims=True)
        m_ref[d] = m_new

    def body(i, carry):
        chunk(0, i, hf_ref)
        chunk(1, nc - 1 - i, hb_ref)
        return carry

    lax.fori_loop(0, nc, body, 0)

    def finish(j, carry):
        sl = pl.ds(pl.multiple_of(j * L, L), L)
        hm = _rms_rows(hf_ref[sl, :] + hb_ref[sl, :])
        h_ref[sl, :] = (hm * mw_ref[...] * _sigmoid(om_ref[sl, :])).astype(h_ref.dtype)
        return carry

    lax.fori_loop(0, nc, finish, 0)


def _mlstm(proj, gates_t, gate_b, mnorm_w, batch, seq, row0, init):
    H, DK, DV = ML_HEADS, ML_DK, ML_DV
    nc = seq // ML_CHUNK
    rb = row0 // seq
    in_specs = [
        pl.BlockSpec(memory_space=pltpu.SMEM),
        pl.BlockSpec((seq, DK), lambda b, h: (rb + b, h)),
        pl.BlockSpec((seq, DK), lambda b, h: (rb + b, H + h)),
        pl.BlockSpec((seq, DV), lambda b, h: (rb + b, (2 * H * DK) // DV + h)),
        pl.BlockSpec((seq, DV), lambda b, h: (rb + b, (2 * H * DK) // DV + H + h)),
        pl.BlockSpec((None, 4 * H, nc, ML_CHUNK), lambda b, h: (b, 0, 0, 0)),
        pl.BlockSpec((1, DV), lambda b, h: (0, h)),
    ]
    args = [gate_b.reshape(-1).astype(F32), proj, proj, proj, proj, gates_t, mnorm_w.reshape(1, H * DV)]
    if init is not None:
        c0, n0, m0, e = init
        nb, ne = c0.shape[:2]
        in_specs += [
            pl.BlockSpec((None, None, 2, None, DK, DV), lambda b, h: (b, e, 0, h, 0, 0)),
            pl.BlockSpec((None, None, 2, None, 1, DK), lambda b, h: (b, e, 0, h, 0, 0)),
            pl.BlockSpec((None, None, 2, None, 1, 1), lambda b, h: (b, e, 0, h, 0, 0)),
        ]
        args += [c0, n0.reshape(nb, ne, 2, H, 1, DK), m0.reshape(nb, ne, 2, H, 1, 1)]
    out_shape = [jax.ShapeDtypeStruct((batch * seq, H * DV), BF16),
                 jax.ShapeDtypeStruct((batch, 2, H, DK, DV), F32),
                 jax.ShapeDtypeStruct((batch, 2, H, 1, DK), F32),
                 jax.ShapeDtypeStruct((batch, 2, H, 1, 1), F32)]
    out_specs = [pl.BlockSpec((seq, DV), lambda b, h: (b, h)),
                 pl.BlockSpec((None, 2, None, DK, DV), lambda b, h: (b, 0, h, 0, 0)),
                 pl.BlockSpec((None, 2, None, 1, DK), lambda b, h: (b, 0, h, 0, 0)),
                 pl.BlockSpec((None, 2, None, 1, 1), lambda b, h: (b, 0, h, 0, 0))]
    hm, c, n, m = pl.pallas_call(
        functools.partial(_mlstm_kernel, seq=seq, has_init=init is not None),
        grid=(batch, H),
        in_specs=in_specs,
        out_specs=out_specs,
        out_shape=out_shape,
        scratch_shapes=[pltpu.VMEM((seq, DV), F32), pltpu.VMEM((seq, DV), F32)],
        compiler_params=_cparams("parallel", "parallel"),
        name="mlstm",
    )(*args)
    return hm, c, n.reshape(batch, 2, H, DK), m.reshape(batch, 2, H)


def _rope_kernel(x_ref, cos_ref, sin_ref, o_ref):
    cos = cos_ref[...]
    sin = sin_ref[...]
    lane = lax.broadcasted_iota(jnp.int32, cos.shape, 1)
    first = (lane & (ATT_HD // 2 - 1)) < (ATT_HD // 4)
    for hh in range(x_ref.shape[1] // ATT_HD):
        x = x_ref[:, hh * ATT_HD:(hh + 1) * ATT_HD]
        swapped = jnp.where(first, pltpu.roll(x, ATT_HD - ATT_HD // 4, axis=1), pltpu.roll(x, ATT_HD // 4, axis=1))
        o_ref[:, hh * ATT_HD:(hh + 1) * ATT_HD] = (x * cos + swapped * sin).astype(o_ref.dtype)


def _rope_tables(seq):
    t = jnp.arange(seq)
    row = (t // GRID_W).astype(F32)
    col = (t % GRID_W).astype(F32)
    nf = ATT_HD // 4
    inv = ROPE_BASE ** (-jnp.arange(nf, dtype=F32) / nf)
    ar = row[:, None] * inv[None, :]
    ac = col[:, None] * inv[None, :]
    cos = jnp.concatenate([jnp.cos(ar), jnp.cos(ar), jnp.cos(ac), jnp.cos(ac)], axis=1)
    sin = jnp.concatenate([-jnp.sin(ar), jnp.sin(ar), -jnp.sin(ac), jnp.sin(ac)], axis=1)
    return cos, sin


def _rope(proj, batch, seq, row0, col0, width, tt=256):
    cos, sin = _rope_tables(seq)
    cw = 512
    nt = seq // tt
    rb, cb = row0 // tt, col0 // cw
    return pl.pallas_call(
        _rope_kernel,
        grid=(batch, nt, width // cw),
        in_specs=[pl.BlockSpec((tt, cw), lambda b, i, j: (rb + b * nt + i, cb + j)),
                  pl.BlockSpec((tt, ATT_HD), lambda b, i, j: (i, 0)),
                  pl.BlockSpec((tt, ATT_HD), lambda b, i, j: (i, 0))],
        out_specs=pl.BlockSpec((tt, cw), lambda b, i, j: (b * nt + i, j)),
        out_shape=jax.ShapeDtypeStruct((batch * seq, width), BF16),
        compiler_params=_cparams("parallel", "parallel", "parallel"),
        name="rope",
    )(proj, cos, sin)


def _attn_kernel(sink_ref, q_ref, kc_ref, vc_ref, *rest, seq, local):
    if local:
        kp_ref, kcur_ref, kn_ref, vp_ref, vcur_ref, vn_ref, o_ref = rest
    else:
        o_ref, = rest
    kvh = pl.program_id(1)
    i = pl.program_id(2)
    tq = q_ref.shape[0]
    scale = ATT_HD ** -0.5
    kc = kc_ref[...].astype(BF16)
    vc = vc_ref[...].astype(BF16)
    if local:
        kl = jnp.concatenate([kp_ref[...], kcur_ref[...], kn_ref[...]], axis=0).astype(BF16)
        vl = jnp.concatenate([vp_ref[...], vcur_ref[...], vn_ref[...]], axis=0).astype(BF16)
        qpos = i * BLK + lax.broadcasted_iota(jnp.int32, (BLK, 3 * BLK), 0)
        kpos = (i - 1) * BLK + lax.broadcasted_iota(jnp.int32, (BLK, 3 * BLK), 1)
        mask = (jnp.abs(qpos - kpos) <= WINDOW) & (kpos >= 0) & (kpos < seq)
    for g in range(ATT_GROUP):
        q = q_ref[:, g * ATT_HD:(g + 1) * ATT_HD].astype(BF16)
        sk = sink_ref[kvh * ATT_GROUP + g]
        s = _dot_nt(q, kc) * scale
        m = jnp.maximum(jnp.max(s, axis=1, keepdims=True), sk)
        if local:
            sl = jnp.where(mask, _dot_nt(q, kl) * scale, -jnp.inf)
            m = jnp.maximum(m, jnp.max(sl, axis=1, keepdims=True))
        e = jnp.exp(s - m)
        den = jnp.sum(e, axis=1, keepdims=True) + jnp.exp(sk - m)
        if local:
            el = jnp.exp(sl - m)
            den = den + jnp.sum(el, axis=1, keepdims=True)
        inv = 1.0 / den
        o = _dot((e * inv).astype(BF16), vc)
        if local:
            o = o + _dot((el * inv).astype(BF16), vl)
        o_ref[:, g * ATT_HD:(g + 1) * ATT_HD] = o.astype(o_ref.dtype)
    del tq


def _attention(sink, q, q_rb, q_cb, kc, kc_spec, vc, vc_spec, batch, seq, tq, local=None):
    nq = seq // tq
    gw = ATT_GROUP * ATT_HD
    in_specs = [pl.BlockSpec(memory_space=pltpu.SMEM),
                pl.BlockSpec((tq, gw), lambda b, h, i: (q_rb + b * nq + i, q_cb + h)),
                kc_spec, vc_spec]
    args = [sink.astype(F32), q, kc, vc]
    if local is not None:
        k_arr, k_rb, k_cb, v_arr, v_rb, v_cb = local
        nb = seq // BLK

        def blk(rb0, cb0, off):
            def index(b, h, i):
                return (rb0 + b * nb + jnp.clip(i + off, 0, nb - 1), cb0 + h)
            return pl.BlockSpec((BLK, ATT_HD), index)

        in_specs += [blk(k_rb, k_cb, -1), blk(k_rb, k_cb, 0), blk(k_rb, k_cb, 1),
                     blk(v_rb, v_cb, -1), blk(v_rb, v_cb, 0), blk(v_rb, v_cb, 1)]
        args += [k_arr, k_arr, k_arr, v_arr, v_arr, v_arr]
    return pl.pallas_call(
        functools.partial(_attn_kernel, seq=seq, local=local is not None),
        grid=(batch, ATT_KV_HEADS, nq),
        in_specs=in_specs,
        out_specs=pl.BlockSpec((tq, gw), lambda b, h, i: (b * nq + i, h)),
        out_shape=jax.ShapeDtypeStruct((batch * seq, ATT_HEADS * ATT_HD), BF16),
        compiler_params=_cparams("parallel", "parallel", "parallel"),
        name="attention",
    )(*args)


def _conv_kernel(x_ref, w_ref, o_ref, *, seq, l2norm, q_blocks):
    x = x_ref[...]
    t = lax.broadcasted_iota(jnp.int32, (seq, 1), 0)
    half = CONV_W // 2
    acc = x * w_ref[half:half + 1, :]
    for j in range(CONV_W):
        if j == half:
            continue
        off = j - half
        shifted = pltpu.roll(x, (-off) % seq, axis=0)
        ok = (t + off >= 0) & (t + off < seq)
        acc = acc + jnp.where(ok, shifted, 0.0) * w_ref[j:j + 1, :]
    y = acc * _sigmoid(acc)
    if not l2norm:
        o_ref[...] = y
        return
    scale = jnp.where(pl.program_id(1) < q_blocks, DN_DK ** -0.5, 1.0)
    for hh in range(x.shape[1] // DN_DK):
        seg = y[:, hh * DN_DK:(hh + 1) * DN_DK]
        seg = seg * lax.rsqrt(jnp.sum(seg * seg, axis=1, keepdims=True) + EPS)
        o_ref[:, hh * DN_DK:(hh + 1) * DN_DK] = seg * scale


def _conv_silu(proj, conv_w, batch, seq, row0, col0, width, l2norm):
    cw = 512
    rb, cb = row0 // seq, col0 // cw
    q_blocks = (DN_QK_HEADS * DN_DK) // cw
    return pl.pallas_call(
        functools.partial(_conv_kernel, seq=seq, l2norm=l2norm, q_blocks=q_blocks),
        grid=(batch, width // cw),
        in_specs=[pl.BlockSpec((seq, cw), lambda b, j: (rb + b, cb + j)),
                  pl.BlockSpec((CONV_W, cw), lambda b, j: (0, cb + j))],
        out_specs=pl.BlockSpec((seq, cw), lambda b, j: (b, j)),
        out_shape=jax.ShapeDtypeStruct((batch * seq, width), F32),
        compiler_params=_cparams("parallel", "parallel"),
        name="conv_silu",
    )(proj, conv_w)


def _delta_kernel(alog_ref, dtb_ref, q_ref, k_ref, v_ref, z_ref, g_ref, dw_ref, *rest, seq, has_init):
    if has_init:
        s0_ref, o_ref, s_ref, of_ref, ob_ref = rest
    else:
        o_ref, s_ref, of_ref, ob_ref = rest
    L, GL, H = DN_CHUNK, DN_GROUP, DN_V_HEADS
    ng = seq // GL
    cpg = GL // L
    hv = pl.program_id(1)
    r = lax.broadcasted_iota(jnp.int32, (GL, GL), 0)
    c = lax.broadcasted_iota(jnp.int32, (GL, GL), 1)
    shift = int(np.log2(L))
    same = lax.shift_right_logical(r, shift) == lax.shift_right_logical(c, shift)
    eye = r == c
    eye_f = jnp.where(eye, 1.0, 0.0)
    incl = (same & (c <= r), same & (c >= r))
    strict = (same & (c < r), same & (c > r))

    def to_col(row):
        return jnp.sum(jnp.where(eye, jnp.broadcast_to(row, (GL, GL)), 0.0), axis=1, keepdims=True)

    def to_row(col):
        return jnp.sum(jnp.where(eye, jnp.broadcast_to(col, (GL, GL)), 0.0), axis=0, keepdims=True)

    if has_init:
        s_ref[...] = s0_ref[...]
    else:
        s_ref[...] = jnp.zeros_like(s_ref)

    def group(d, j, out_ref):
        base = pl.multiple_of(j * GL, GL)
        sl = pl.ds(base, GL)
        q = q_ref[sl, :]
        k = k_ref[sl, :]
        v = v_ref[sl, :]
        gidx = d * H + hv
        a_row = g_ref[gidx, pl.ds(j, 1), :]
        b_row = g_ref[2 * H + gidx, pl.ds(j, 1), :]
        g_row = -jnp.exp(alog_ref[gidx]) * _softplus(a_row + dtb_ref[gidx])
        beta_col = to_col(_sigmoid(b_row))
        gmat = jnp.broadcast_to(g_row, (GL, GL))
        gcs_col = jnp.sum(jnp.where(incl[d], gmat, 0.0), axis=1, keepdims=True)
        tot_col = jnp.sum(jnp.where(same, gmat, 0.0), axis=1, keepdims=True)
        gcs_row = to_row(gcs_col)
        decay = jnp.exp(jnp.where(incl[d], gcs_col - gcs_row, -jnp.inf))
        kb = k * beta_col
        kbf = k.astype(BF16)
        a_mat = jnp.where(strict[d], _dot_nt(kb.astype(BF16), kbf) * decay, 0.0)
        eg = jnp.exp(gcs_col)
        rhs = jnp.concatenate([v * beta_col, kb * eg], axis=1)
        tinv = eye_f - a_mat
        apow = _dot3(a_mat, a_mat)
        for _ in range(shift - 2):
            tinv = tinv + _dot3(tinv, apow)
            apow = _dot3(apow, apow)
        tinv = tinv + _dot3(tinv, apow)
        sol = _dot3(tinv, rhs)
        u = sol[:, :DN_DV]
        w = sol[:, DN_DV:].astype(BF16)
        intra = (_dot_nt(q.astype(BF16), kbf) * decay).astype(BF16)
        qe = (q * eg).astype(BF16)
        kd = (k * jnp.exp(tot_col - gcs_col)).astype(BF16)
        s = s_ref[d]
        for ci in (range(cpg) if d == 0 else reversed(range(cpg))):
            lo, hi = ci * L, (ci + 1) * L
            sb = s.astype(BF16)
            v_new = u[lo:hi] - _dot(w[lo:hi], sb)
            vb = v_new.astype(BF16)
            out_ref[pl.ds(base + lo, L), :] = _dot(qe[lo:hi], sb) + _dot(intra[lo:hi, lo:hi], vb)
            s = s * jnp.exp(tot_col[lo:lo + 1, :]) + _dot_tn(kd[lo:hi], vb)
        s_ref[d] = s

    def body(i, carry):
        group(0, i, of_ref)
        group(1, ng - 1 - i, ob_ref)
        return carry

    lax.fori_loop(0, ng, body, 0)
    o = _rms_rows(of_ref[...] + ob_ref[...]) * dw_ref[...]
    z = z_ref[...]
    o_ref[...] = (o * (z * _sigmoid(z))).astype(o_ref.dtype)


def _delta(qkv, proj, z_cb, gates_t, a_log, dt_bias, dnorm_w, batch, seq, row0, init):
    H, DK, DV, GL = DN_V_HEADS, DN_DK, DN_DV, DN_GROUP
    rep = DN_V_HEADS // DN_QK_HEADS
    ng = seq // GL
    rb = row0 // seq
    in_specs = [
        pl.BlockSpec(memory_space=pltpu.SMEM),
        pl.BlockSpec(memory_space=pltpu.SMEM),
        pl.BlockSpec((seq, DK), lambda b, h: (b, h // rep)),
        pl.BlockSpec((seq, DK), lambda b, h: (b, DN_QK_HEADS + h // rep)),
        pl.BlockSpec((seq, DV), lambda b, h: (b, 2 * DN_QK_HEADS + h)),
        pl.BlockSpec((seq, DV), lambda b, h: (rb + b, z_cb + h)),
        pl.BlockSpec((None, 4 * H, ng, GL), lambda b, h: (b, 0, 0, 0)),
        pl.BlockSpec((1, DV), lambda b, h: (0, 0)),
    ]
    args = [a_log.reshape(-1).astype(F32), dt_bias.reshape(-1).astype(F32), qkv, qkv, qkv, proj, gates_t,
            dnorm_w.reshape(1, DV)]
    if init is not None:
        s0, o_idx = init
        in_specs.append(pl.BlockSpec((None, None, 2, None, DK, DV), lambda b, h: (b, o_idx, 0, h, 0, 0)))
        args.append(s0)
    return pl.pallas_call(
        functools.partial(_delta_kernel, seq=seq, has_init=init is not None),
        grid=(batch, H),
        in_specs=in_specs,
        out_specs=[pl.BlockSpec((seq, DV), lambda b, h: (b, h)),
                   pl.BlockSpec((None, 2, None, DK, DV), lambda b, h: (b, 0, h, 0, 0))],
        out_shape=[jax.ShapeDtypeStruct((batch * seq, H * DV), BF16),
                   jax.ShapeDtypeStruct((batch, 2, H, DK, DV), F32)],
        scratch_shapes=[pltpu.VMEM((seq, DV), F32), pltpu.VMEM((seq, DV), F32)],
        compiler_params=_cparams("parallel", "parallel"),
        name="delta",
    )(*args)


def _gates_transposed(g, batch, seq, row0, group):
    c = g.shape[1]
    gs = g[row0:row0 + batch * seq].reshape(batch, seq, c)
    return gs.transpose(0, 2, 1).reshape(batch, c, seq // group, group)


def _even_mixer(h, w_main, w_gate, gate_b, mnorm_w, sink, w_out, e, dims, cache_k, cache_v, st_c, st_n, st_m):
    B, S, DB, DS = dims
    mp = B * S
    proj = _matmul(h, w_main, F32, 1024, 1024)
    gm = _matmul(h, w_gate, F32, 1024, LANES)[:, :4 * ML_HEADS]
    qa_c, ka_c, va_c = 6144, 8192, 8704
    gt_p = _gates_transposed(gm, B, S, 0, ML_CHUNK)
    hm_p, c_p, n_p, m_p = _mlstm(proj, gt_p, gate_b, mnorm_w, B, S, 0, None)
    kv_spec = lambda col: pl.BlockSpec((S, ATT_HD), lambda b, hh, i: (b, col // ATT_HD + hh))
    ha_p = _attention(sink, proj, 0, qa_c // 512, proj, kv_spec(ka_c), proj, kv_spec(va_c), B, S, S)
    gt_s = _gates_transposed(gm, DB, DS, mp, ML_CHUNK)
    hm_s, _, _, _ = _mlstm(proj, gt_s, gate_b, mnorm_w, DB, DS, mp, (st_c, st_n, st_m, e))
    qk_rot = _rope(proj, DB, DS, mp, qa_c, ATT_HEADS * ATT_HD + ATT_KV_HEADS * ATT_HD)
    pl_ = cache_k.shape[2]
    ck = cache_k.reshape(cache_k.shape[0], cache_k.shape[1], pl_, ATT_KV_HEADS * ATT_HD)
    cv = cache_v.reshape(cache_v.shape[0], cache_v.shape[1], pl_, ATT_KV_HEADS * ATT_HD)
    cache_spec = pl.BlockSpec((None, None, pl_, ATT_HD), lambda b, hh, i: (b, e, 0, hh))
    ha_s = _attention(sink, qk_rot, 0, 0, ck, cache_spec, cv, cache_spec, DB, DS, BLK,
                      local=(qk_rot, 0, (ATT_HEADS * ATT_HD) // ATT_HD, proj, mp // BLK, va_c // ATT_HD))
    mix = jnp.concatenate([jnp.concatenate([hm_p, ha_p], axis=1), jnp.concatenate([hm_s, ha_s], axis=1)], axis=0)
    y = _matmul(mix, w_out, F32, 1024, 1024)
    new_k = proj[:mp, ka_c:va_c].reshape(B, S, ATT_KV_HEADS, ATT_HD)
    new_v = proj[:mp, va_c:va_c + ATT_KV_HEADS * ATT_HD].reshape(B, S, ATT_KV_HEADS, ATT_HD)
    return y, (new_k, new_v, c_p, n_p, m_p)


def _odd_mixer(h, w_main, w_gate, conv_w, a_log, dt_bias, dnorm_w, w_out, o, dims, st_delta):
    B, S, DB, DS = dims
    mp = B * S
    conv_ch = 2 * DN_QK_HEADS * DN_DK + DN_V_HEADS * DN_DV
    qk_ch = 2 * DN_QK_HEADS * DN_DK
    proj = _matmul(h, w_main, F32, 1024, 1024)
    ab = _matmul(h, w_gate, F32, 1024, LANES)
    outs, new_s = [], None
    for (bb, ss, row0, init) in ((B, S, 0, None), (DB, DS, mp, (st_delta, o))):
        qk = _conv_silu(proj, conv_w, bb, ss, row0, 0, qk_ch, True)
        vv = _conv_silu(proj, conv_w, bb, ss, row0, qk_ch, conv_ch - qk_ch, False)
        qkv = jnp.concatenate([qk, vv], axis=1)
        gt = _gates_transposed(ab, bb, ss, row0, DN_GROUP)
        out, s_fin = _delta(qkv, proj, conv_ch // DN_DV, gt, a_log, dt_bias, dnorm_w, bb, ss, row0, init)
        outs.append(out)
        if init is None:
            new_s = s_fin
    y = _matmul(jnp.concatenate(outs, axis=0), w_out, F32, 1024, 1024)
    return y, new_s


def kernel(x_prompt, x_sample, cache_attn_k, cache_attn_v, state_mlstm_c, state_mlstm_n, state_mlstm_m, state_delta, c, c_ctx, w_mod, b_mod, norm_w, even_w_in, even_gate_b, even_mnorm_w, even_sink, even_w_out, ffn_w_gate, ffn_w_up, ffn_w_down, odd_w_in, odd_conv_w, odd_a_log, odd_dt_bias, odd_dnorm_w, odd_w_out, moe_router, moe_w_gate, moe_w_up, moe_w_down):
    B, S, D = x_prompt.shape
    DB, DS, _ = x_sample.shape
    depth = w_mod.shape[0]
    dims = (B, S, DB, DS)
    mp = B * S
    rows = _Rows(mp, DS, 256)

    x = jnp.concatenate([x_prompt.reshape(mp, D), x_sample.reshape(DB * DS, D)], axis=0)
    n_mod = 16
    cc = jnp.zeros((n_mod, D), F32).at[0].set(c_ctx).at[1:1 + DB].set(c)
    mod_all = _mod_table(cc, w_mod, b_mod)

    ml_cols = 2 * ML_HEADS * ML_DK + 2 * ML_HEADS * ML_DV
    new_k, new_v, new_c, new_n, new_m, new_s = [], [], [], [], [], []
    for li in range(depth):
        mod = mod_all[li].reshape(n_mod * 6, 1, D)
        h = _norm_modulate(x, norm_w[li, 0], mod, rows, 0, 1)
        if li % 2 == 0:
            e = li // 2
            w_in = even_w_in[e]
            w_main = jnp.concatenate([w_in[:, :ml_cols], w_in[:, ml_cols + 4 * ML_HEADS:]], axis=1).astype(BF16)
            w_gate = jnp.zeros((D, LANES), BF16).at[:, :4 * ML_HEADS].set(
                w_in[:, ml_cols:ml_cols + 4 * ML_HEADS].astype(BF16))
            y, (k_e, v_e, c_e, n_e, m_e) = _even_mixer(
                h, w_main, w_gate, even_gate_b[e], even_mnorm_w[e], even_sink[e], even_w_out[e].astype(BF16), e, dims,
                cache_attn_k, cache_attn_v, state_mlstm_c, state_mlstm_n, state_mlstm_m)
            new_k.append(k_e)
            new_v.append(v_e)
            new_c.append(c_e)
            new_n.append(n_e)
            new_m.append(m_e)
        else:
            o = li // 2
            w_in = odd_w_in[o]
            main = w_in.shape[1] - LANES
            y, s_o = _odd_mixer(h, w_in[:, :main].astype(BF16), w_in[:, main:].astype(BF16), odd_conv_w[o],
                                odd_a_log[o], odd_dt_bias[o], odd_dnorm_w[o], odd_w_out[o].astype(BF16), o, dims,
                                state_delta)
            new_s.append(s_o)
        x = _gated_residual(x, y, norm_w[li, 1], mod, rows, 2)
        h = _norm_modulate(x, norm_w[li, 2], mod, rows, 3, 4)
        if li % 2 == 0:
            e = li // 2
            act = _swiglu_up(h, ffn_w_gate[e].astype(BF16)[None], ffn_w_up[e].astype(BF16)[None], 1024, 256)
            y = _matmul(act, ffn_w_down[e].astype(BF16), F32, 512, 1024, tk=act.shape[1] // 2)
        else:
            o = li // 2
            gates = _moe_gates(h, moe_router[o])
            act = _swiglu_up(h, moe_w_gate[o].astype(BF16), moe_w_up[o].astype(BF16), 1024, 512)
            y = _moe_down(act, moe_w_down[o].astype(BF16), gates, 1024, 1024)
        x = _gated_residual(x, y, norm_w[li, 3], mod, rows, 5)

    return (x[:mp].reshape(B, S, D), x[mp:].reshape(DB, DS, D),
            jnp.stack(new_k, axis=1), jnp.stack(new_v, axis=1), jnp.stack(new_c, axis=1),
            jnp.stack(new_n, axis=1), jnp.stack(new_m, axis=1), jnp.stack(new_s, axis=1))
```
